```python
import math
import jax, jax.numpy as jnp
from jax import lax
import numpy as np

D_MODEL = 1024
BATCH = 8
SEQ = 8192
DEPTH = 2
DEC_BATCH = 8
DEC_SEQ = 2048
PAST_LEN = 128

R_HEADS = 4
R_DK = 128
R_DV = 256
D_HEADS = 4
D_DH = 128
D_DV = 256
RET_QK = R_HEADS * R_DK
RET_V = R_HEADS * R_DV
DIFF_QK = D_HEADS * 2 * D_DH
DIFF_V = D_HEADS * D_DV
IN_WIDTH = 2 * RET_QK + 2 * RET_V + 2 * DIFF_QK + DIFF_V + 2 * D_MODEL
CHUNK = 128
Q_BLOCK = 128
ROPE_THETA = 10000.0
N_GROUPS = 4
EXPERTS_PER_GROUP = 8
N_EXPERTS = N_GROUPS * EXPERTS_PER_GROUP
TOP_K = 2
D_FF_EXPERT = 512
ALPHA = (2.0 * DEPTH) ** 0.25
BETA = (8.0 * DEPTH) ** -0.25
EPS = 1e-5

kernel_name = "hybrid_retention_diffattn_hmoe_encoder"


def layer_norm(x, g, b):
    xf = x.astype(jnp.float32)
    mu = jnp.mean(xf, axis=-1, keepdims=True)
    var = jnp.mean(jnp.square(xf - mu), axis=-1, keepdims=True)
    y = (xf - mu) * lax.rsqrt(var + EPS) * g.astype(jnp.float32) + b.astype(jnp.float32)
    return y.astype(x.dtype)


def rope_tables(s, d):
    pos = jnp.arange(s, dtype=jnp.float32)
    inv = ROPE_THETA ** (-jnp.arange(0, d, 2, dtype=jnp.float32) / d)
    ang = pos[:, None] * inv[None, :]
    return jnp.cos(ang), jnp.sin(ang)


def apply_rope(x, cos, sin):
    half = x.shape[-1] // 2
    c = cos[:, None, :]
    s = sin[:, None, :]
    xf = x.astype(jnp.float32)
    x1, x2 = xf[..., :half], xf[..., half:]
    out = jnp.concatenate([x1 * c - x2 * s, x2 * c + x1 * s], axis=-1)
    return out.astype(x.dtype)


def retention_one_direction(q, k, v, log_g, include_diag):
    b, s, h, dk = q.shape
    dv = v.shape[-1]
    n = s // CHUNK
    idx = jnp.arange(CHUNK, dtype=jnp.float32)
    diff = idx[:, None] - idx[None, :]
    mask = (diff >= 0) if include_diag else (diff > 0)
    intra = jnp.where(mask[None], jnp.exp(log_g[:, None, None] * jnp.maximum(diff, 0.0)[None]), 0.0)
    xi = jnp.exp(log_g[None, :] * (idx[:, None] + 1.0))
    zeta = jnp.exp(log_g[None, :] * (CHUNK - 1.0 - idx[:, None]))
    chunk_decay = jnp.exp(log_g * CHUNK)

    def to_chunks(t):
        return t.reshape(b, n, CHUNK, h, t.shape[-1]).swapaxes(0, 1)

    def body(state, inp):
        qc, kc, vc = inp
        sc = jnp.einsum('bihd,bjhd->bhij', qc, kc) * intra[None]
        o = jnp.einsum('bhij,bjhe->bihe', sc, vc)
        o = o + jnp.einsum('bihd,bhde->bihe', qc * xi[None, :, :, None], state)
        state = state * chunk_decay[None, :, None, None] + jnp.einsum('bjhd,bjhe->bhde', kc * zeta[None, :, :, None], vc)
        return state, o

    state0 = jnp.zeros((b, h, dk, dv), dtype=jnp.float32)
    _, o = lax.scan(body, state0, (to_chunks(q), to_chunks(k), to_chunks(v)))
    return o.swapaxes(0, 1).reshape(b, s, h, dv)


def bidirectional_retention(q, k, v, log_decay):
    log_f = -jnp.abs(log_decay[0].astype(jnp.float32))
    log_b = -jnp.abs(log_decay[1].astype(jnp.float32))
    o_f = retention_one_direction(q, k, v, log_f, True)
    o_b = retention_one_direction(q[:, ::-1], k[:, ::-1], v[:, ::-1], log_b, False)[:, ::-1]
    return o_f + o_b


def diff_attention(q1, q2, k1, k2, v, lam):
    b, s, h, d = q1.shape
    nq = s // Q_BLOCK
    scale = 1.0 / math.sqrt(d)
    qb = jnp.stack([q1, q2], axis=0).reshape(2, b, nq, Q_BLOCK, h, d).transpose(2, 0, 1, 3, 4, 5)

    def block(qblk):
        s1 = jnp.einsum('bqhd,bkhd->bhqk', qblk[0], k1).astype(jnp.float32) * scale
        s2 = jnp.einsum('bqhd,bkhd->bhqk', qblk[1], k2).astype(jnp.float32) * scale
        a = jax.nn.softmax(s1, axis=-1) - lam * jax.nn.softmax(s2, axis=-1)
        return jnp.einsum('bhqk,bkhe->bqhe', a, v.astype(jnp.float32))

    out = lax.map(block, qb)
    return out.transpose(1, 0, 2, 3, 4).reshape(b, s, h, v.shape[-1])


def hier_moe(x, w_rg, w_re, w_gu, w_dn):
    b, s, d = x.shape
    xt = x.reshape(b * s, d)
    g_prob = jax.nn.softmax((xt @ w_rg).astype(jnp.float32), axis=-1)
    gp, gsel = lax.top_k(g_prob, 1)
    e_logits = (xt @ w_re).astype(jnp.float32).reshape(-1, N_GROUPS, EXPERTS_PER_GROUP)
    e_sel = jnp.take_along_axis(e_logits, gsel[:, :, None], axis=1)[:, 0]
    e_prob = jax.nn.softmax(e_sel, axis=-1)
    ev, eidx = lax.top_k(e_prob, TOP_K)
    w = gp * ev / jnp.sum(ev, axis=-1, keepdims=True)
    eid = gsel * EXPERTS_PER_GROUP + eidx
    combine = jnp.sum(jax.nn.one_hot(eid, N_EXPERTS, dtype=jnp.float32) * w[..., None], axis=1)
    y = jnp.zeros((b * s, d), dtype=jnp.float32)
    for e in range(N_EXPERTS):
        hgu = xt @ w_gu[e]
        a, gate = hgu[:, :D_FF_EXPERT], hgu[:, D_FF_EXPERT:]
        y = y + combine[:, e:e + 1] * ((jax.nn.silu(a) * gate) @ w_dn[e]).astype(jnp.float32)
    return y.astype(x.dtype).reshape(b, s, d)


def encoder_layer(x, layer_idx, cos, sin, w_in, ret_log_decay, ret_gn_gain, diff_lambda, diff_subln_gain,
                  w_ret_branch, w_diff_branch, w_out, ln1_g, ln1_b, router_group, router_expert,
                  w_gate_up, w_down, ln2_g, ln2_b):
    b, s, _ = x.shape
    h = x @ w_in
    o = 0
    rq = h[..., o:o + RET_QK].reshape(b, s, R_HEADS, R_DK); o += RET_QK
    rk = h[..., o:o + RET_QK].reshape(b, s, R_HEADS, R_DK); o += RET_QK
    rv = h[..., o:o + RET_V].reshape(b, s, R_HEADS, R_DV); o += RET_V
    rg = h[..., o:o + RET_V]; o += RET_V
    dq = h[..., o:o + DIFF_QK].reshape(b, s, D_HEADS * 2, D_DH); o += DIFF_QK
    dk = h[..., o:o + DIFF_QK].reshape(b, s, D_HEADS * 2, D_DH); o += DIFF_QK
    dv = h[..., o:o + DIFF_V].reshape(b, s, D_HEADS, D_DV); o += DIFF_V
    gate_a = h[..., o:o + D_MODEL]; o += D_MODEL
    gate_b = h[..., o:o + D_MODEL]

    rq = apply_rope(rq, cos, sin)
    rk = apply_rope(rk, cos, sin) * (R_DK ** -0.5)
    ro = bidirectional_retention(rq, rk, rv, ret_log_decay)
    mu = jnp.mean(ro, axis=-1, keepdims=True)
    var = jnp.mean(jnp.square(ro - mu), axis=-1, keepdims=True)
    ro = ((ro - mu) * lax.rsqrt(var + EPS)).reshape(b, s, RET_V) * ret_gn_gain.astype(jnp.float32)
    ret_out = (jax.nn.silu(rg.astype(jnp.float32)) * ro).astype(x.dtype)

    dq = apply_rope(dq, cos, sin).reshape(b, s, D_HEADS, 2, D_DH)
    dk = apply_rope(dk, cos, sin).reshape(b, s, D_HEADS, 2, D_DH)
    lam_init = 0.8 - 0.6 * math.exp(-0.3 * layer_idx)
    lf = diff_lambda.astype(jnp.float32)
    lam = jnp.exp(jnp.sum(lf[0] * lf[1])) - jnp.exp(jnp.sum(lf[2] * lf[3])) + lam_init
    do = diff_attention(dq[..., 0, :], dq[..., 1, :], dk[..., 0, :], dk[..., 1, :], dv, lam)
    do = do * lax.rsqrt(jnp.mean(jnp.square(do), axis=-1, keepdims=True) + EPS) * diff_subln_gain.astype(jnp.float32)
    diff_out = (do * (1.0 - lam_init)).reshape(b, s, DIFF_V).astype(x.dtype)

    merged = jax.nn.sigmoid(gate_a) * (ret_out @ w_ret_branch) + jax.nn.sigmoid(gate_b) * (diff_out @ w_diff_branch)
    mix = merged @ w_out
    x = layer_norm(ALPHA * x + mix, ln1_g, ln1_b)

    x = layer_norm(ALPHA * x + hier_moe(x, router_group, router_expert, w_gate_up, w_down), ln2_g, ln2_b)
    return x


def trunk(x, w_in, ret_log_decay, ret_gn_gain, diff_lambda, diff_subln_gain, w_ret_branch, w_diff_branch,
          w_out, ln1_g, ln1_b, router_group, router_expert, w_gate_up, w_down, ln2_g, ln2_b):
    cos, sin = rope_tables(x.shape[1], R_DK)
    for l in range(DEPTH):
        x = encoder_layer(x, l, cos, sin, w_in[l], ret_log_decay[l], ret_gn_gain[l], diff_lambda[l],
                          diff_subln_gain[l], w_ret_branch[l], w_diff_branch[l], w_out[l], ln1_g[l], ln1_b[l],
                          router_group[l], router_expert[l], w_gate_up[l], w_down[l], ln2_g[l], ln2_b[l])
    return x


def setup_inputs(seed: int = 0) -> dict:
    key = jax.random.key(seed)
    ks = jax.random.split(key, 24)
    f32 = jnp.float32
    nrm = lambda k, shape, scale: jax.random.normal(k, shape, f32) * scale
    x_prompt = nrm(ks[0], (BATCH, SEQ, D_MODEL), 1.0)
    x_sample = nrm(ks[1], (DEC_BATCH, DEC_SEQ, D_MODEL), 1.0)
    sd = D_MODEL ** -0.5
    w_in = jnp.concatenate([
        nrm(ks[2], (DEPTH, D_MODEL, 2 * RET_QK), sd),
        nrm(ks[3], (DEPTH, D_MODEL, RET_V), sd * BETA),
        nrm(ks[4], (DEPTH, D_MODEL, RET_V), sd),
        nrm(ks[5], (DEPTH, D_MODEL, 2 * DIFF_QK), sd),
        nrm(ks[6], (DEPTH, D_MODEL, DIFF_V), sd * BETA),
        nrm(ks[7], (DEPTH, D_MODEL, 2 * D_MODEL), sd),
    ], axis=-1)
    base = jnp.log(1.0 - 2.0 ** (-5.0 - jnp.arange(R_HEADS, dtype=f32)))
    ret_log_decay = base[None, None, :] * (1.0 + 0.1 * jax.random.uniform(ks[8], (DEPTH, 2, R_HEADS), f32))
    ret_gn_gain = 1.0 + nrm(ks[9], (DEPTH, RET_V), 0.02)
    diff_lambda = nrm(ks[10], (DEPTH, 4, D_DH), 0.1)
    diff_subln_gain = 1.0 + nrm(ks[11], (DEPTH, D_DV), 0.02)
    w_ret_branch = nrm(ks[12], (DEPTH, RET_V, D_MODEL), RET_V ** -0.5 * BETA)
    w_diff_branch = nrm(ks[13], (DEPTH, DIFF_V, D_MODEL), DIFF_V ** -0.5 * BETA)
    w_out = nrm(ks[14], (DEPTH, D_MODEL, D_MODEL), sd * BETA)
    ln1_g = 1.0 + nrm(ks[15], (DEPTH, D_MODEL), 0.02)
    ln1_b = nrm(ks[16], (DEPTH, D_MODEL), 0.02)
    router_group = nrm(ks[17], (DEPTH, D_MODEL, N_GROUPS), sd)
    router_expert = nrm(ks[18], (DEPTH, D_MODEL, N_EXPERTS), sd)
    w_gate_up = nrm(ks[19], (DEPTH, N_EXPERTS, D_MODEL, 2 * D_FF_EXPERT), sd * BETA)
    w_down = nrm(ks[20], (DEPTH, N_EXPERTS, D_FF_EXPERT, D_MODEL), D_FF_EXPERT ** -0.5 * BETA)
    ln2_g = 1.0 + nrm(ks[21], (DEPTH, D_MODEL), 0.02)
    ln2_b = nrm(ks[22], (DEPTH, D_MODEL), 0.02)
    return {"x_prompt": x_prompt, "x_sample": x_sample, "w_in": w_in, "ret_log_decay": ret_log_decay,
            "ret_gn_gain": ret_gn_gain, "diff_lambda": diff_lambda, "diff_subln_gain": diff_subln_gain,
            "w_ret_branch": w_ret_branch, "w_diff_branch": w_diff_branch, "w_out": w_out,
            "ln1_g": ln1_g, "ln1_b": ln1_b, "router_group": router_group, "router_expert": router_expert,
            "w_gate_up": w_gate_up, "w_down": w_down, "ln2_g": ln2_g, "ln2_b": ln2_b}


def reference(x_prompt, x_sample, w_in, ret_log_decay, ret_gn_gain, diff_lambda, diff_subln_gain,
              w_ret_branch, w_diff_branch, w_out, ln1_g, ln1_b, router_group, router_expert,
              w_gate_up, w_down, ln2_g, ln2_b):
    y_prompt = trunk(x_prompt, w_in, ret_log_decay, ret_gn_gain, diff_lambda, diff_subln_gain, w_ret_branch,
                     w_diff_branch, w_out, ln1_g, ln1_b, router_group, router_expert, w_gate_up, w_down,
                     ln2_g, ln2_b)
    y_sample = trunk(x_sample, w_in, ret_log_decay, ret_gn_gain, diff_lambda, diff_subln_gain, w_ret_branch,
                     w_diff_branch, w_out, ln1_g, ln1_b, router_group, router_expert, w_gate_up, w_down,
                     ln2_g, ln2_b)
    return (y_prompt, y_sample)
```

```python
import functools
import math

import jax
import jax.numpy as jnp
from jax import lax
from jax.experimental import pallas as pl
from jax.experimental.pallas import tpu as pltpu

F32 = jnp.float32
BF16 = jnp.bfloat16

D_MODEL = 1024
DEPTH = 2
R_HEADS = 4
R_DK = 128
R_DV = 256
D_HEADS = 4
D_DH = 128
D_DV = 256
RET_QK = R_HEADS * R_DK
RET_V = R_HEADS * R_DV
DIFF_QK = D_HEADS * 2 * D_DH
DIFF_V = D_HEADS * D_DV
IN_WIDTH = 2 * RET_QK + 2 * RET_V + 2 * DIFF_QK + DIFF_V + 2 * D_MODEL
OFF_RQ = 0
OFF_RK = OFF_RQ + RET_QK
OFF_RV = OFF_RK + RET_QK
OFF_RG = OFF_RV + RET_V
OFF_DQ = OFF_RG + RET_V
OFF_DK = OFF_DQ + DIFF_QK
OFF_DV = OFF_DK + DIFF_QK
OFF_GA = OFF_DV + DIFF_V
OFF_GB = OFF_GA + D_MODEL
ROPE_THETA = 10000.0
N_GROUPS = 4
EXPERTS_PER_GROUP = 8
N_EXPERTS = N_GROUPS * EXPERTS_PER_GROUP
D_FF_EXPERT = 512
ALPHA = (2.0 * DEPTH) ** 0.25
EPS = 1e-5

LANES = 128
RET_CHUNK = 128
VMEM_LIMIT = 56 * 1024 * 1024

NT_DIMS = (((1,), (1,)), ((), ()))
TN_DIMS = (((0,), (0,)), ((), ()))


def _cparams(*sem):
    return pltpu.CompilerParams(dimension_semantics=sem, vmem_limit_bytes=VMEM_LIMIT)


PROJ_TM = 1024
PROJ_TN = 1024
ROPE_TILES = tuple(sorted({OFF_RQ // PROJ_TN, OFF_RK // PROJ_TN, OFF_DQ // PROJ_TN, OFF_DK // PROJ_TN,
                           (OFF_DK + DIFF_QK - 1) // PROJ_TN}))


def _inproj_kernel(x_ref, w_ref, cos_ref, sin_ref, scale_ref, o_ref):
    j = pl.program_id(0)
    acc = jnp.dot(x_ref[...], w_ref[...], preferred_element_type=F32)
    is_rope = functools.reduce(jnp.logical_or, [j == t for t in ROPE_TILES])

    @pl.when(is_rope)
    def _():
        cos = cos_ref[...]
        sin = sin_ref[...]
        for g in range(PROJ_TN // LANES):
            sl = slice(g * LANES, (g + 1) * LANES)
            a = acc[:, sl]
            r = pltpu.roll(a, LANES // 2, 1)
            o_ref[:, sl] = ((a * cos + r * sin) * scale_ref[:, sl]).astype(o_ref.dtype)

    @pl.when(jnp.logical_not(is_rope))
    def _():
        o_ref[...] = acc.astype(o_ref.dtype)


def _inproj(xb, w, cos_t, sin_t, colscale, pos_block):
    t = xb.shape[0]
    grid = (IN_WIDTH // PROJ_TN, t // PROJ_TM)
    return pl.pallas_call(
        _inproj_kernel,
        grid=grid,
        in_specs=[
            pl.BlockSpec((PROJ_TM, D_MODEL), lambda j, i: (i, 0)),
            pl.BlockSpec((D_MODEL, PROJ_TN), lambda j, i: (0, j)),
            pl.BlockSpec((PROJ_TM, LANES), lambda j, i: (pos_block(i), 0)),
            pl.BlockSpec((PROJ_TM, LANES), lambda j, i: (pos_block(i), 0)),
            pl.BlockSpec((1, PROJ_TN), lambda j, i: (0, j)),
        ],
        out_specs=pl.BlockSpec((PROJ_TM, PROJ_TN), lambda j, i: (i, j)),
        out_shape=jax.ShapeDtypeStruct((t, IN_WIDTH), BF16),
        compiler_params=_cparams("parallel", "parallel"),
        name="inproj",
    )(xb, w, cos_t, sin_t, colscale)


def _ret_kernel(dec_ref, qf_ref, kf_ref, vf_ref, qb_ref, kb_ref, vb_ref, dmat_ref, xif_ref, ztf_ref,
                xib_ref, ztb_ref, oa_ref, ob_ref, sf_ref, sb_ref, *, is_first):
    c = pl.program_id(0)

    @pl.when(is_first(c))
    def _():
        sf_ref[...] = jnp.zeros_like(sf_ref)
        sb_ref[...] = jnp.zeros_like(sb_ref)

    for hd in range(R_HEADS):
        qs = slice(hd * R_DK, (hd + 1) * R_DK)
        vs = slice(hd * R_DV, (hd + 1) * R_DV)
        q = qf_ref[:, qs]
        k = kf_ref[:, qs]
        v = vf_ref[:, vs]
        s = lax.dot_general(q, k, NT_DIMS, preferred_element_type=F32)
        p = (s * dmat_ref[hd]).astype(BF16)
        o = jnp.dot(p, v, preferred_element_type=F32)
        st = sf_ref[hd]
        o = o + xif_ref[hd] * jnp.dot(q, st.astype(BF16), preferred_element_type=F32)
        kz = (k.astype(F32) * ztf_ref[hd]).astype(BF16)
        sf_ref[hd] = st * dec_ref[0, hd] + lax.dot_general(kz, v, TN_DIMS, preferred_element_type=F32)
        oa_ref[:, vs] = o
        q = qb_ref[:, qs]
        k = kb_ref[:, qs]
        v = vb_ref[:, vs]
        st = sb_ref[hd]
        ob_ref[:, vs] = xib_ref[hd] * jnp.dot(q, st.astype(BF16), preferred_element_type=F32)
        kz = (k.astype(F32) * ztb_ref[hd]).astype(BF16)
        sb_ref[hd] = st * dec_ref[1, hd] + lax.dot_general(kz, v, TN_DIMS, preferred_element_type=F32)


def _retention(h, log_decay, seqs):
    t = h.shape[0]
    c_len = RET_CHUNK
    n_chunks = t // c_len
    bounds = []
    start = 0
    for nseq, slen in seqs:
        per = slen // c_len
        bounds.append((start, start + nseq * per, per))
        start += nseq * per

    def is_first(c):
        out = False
        for lo, hi, per in bounds:
            out = jnp.logical_or(out, jnp.logical_and(jnp.logical_and(c >= lo, c < hi), (c - lo) % per == 0))
        return out

    def rev(c):
        out = c
        for lo, hi, per in bounds:
            r = lo + ((c - lo) // per) * per + (per - 1 - (c - lo) % per)
            out = jnp.where(jnp.logical_and(c >= lo, c < hi), r, out)
        return out

    lf = -jnp.abs(log_decay[0].astype(F32))
    lb = -jnp.abs(log_decay[1].astype(F32))
    idx = jnp.arange(c_len, dtype=F32)
    diff = idx[:, None] - idx[None, :]
    dmat = jnp.where(diff[None] >= 0,
                     jnp.exp(lf[:, None, None] * jnp.maximum(diff, 0.0)[None]),
                     jnp.exp(lb[:, None, None] * jnp.maximum(-diff, 0.0)[None]))
    ones_v = jnp.ones((1, 1, R_DV), F32)
    ones_k = jnp.ones((1, 1, R_DK), F32)
    xif = jnp.exp(lf[:, None, None] * (idx[None, :, None] + 1.0)) * ones_v
    ztf = jnp.exp(lf[:, None, None] * (c_len - 1.0 - idx[None, :, None])) * ones_k
    xib = jnp.exp(lb[:, None, None] * (c_len - idx[None, :, None])) * ones_v
    ztb = jnp.exp(lb[:, None, None] * idx[None, :, None]) * ones_k
    dec = jnp.stack([jnp.exp(lf * c_len), jnp.exp(lb * c_len)])

    qk_w = RET_QK
    full = lambda shape: pl.BlockSpec(shape, lambda c: (0,) * len(shape))
    kern = functools.partial(_ret_kernel, is_first=is_first)
    return pl.pallas_call(
        kern,
        grid=(n_chunks,),
        in_specs=[
            pl.BlockSpec(memory_space=pltpu.SMEM),
            pl.BlockSpec((c_len, qk_w), lambda c: (c, OFF_RQ // qk_w)),
            pl.BlockSpec((c_len, qk_w), lambda c: (c, OFF_RK // qk_w)),
            pl.BlockSpec((c_len, RET_V), lambda c: (c, OFF_RV // RET_V)),
            pl.BlockSpec((c_len, qk_w), lambda c: (rev(c), OFF_RQ // qk_w)),
            pl.BlockSpec((c_len, qk_w), lambda c: (rev(c), OFF_RK // qk_w)),
            pl.BlockSpec((c_len, RET_V), lambda c: (rev(c), OFF_RV // RET_V)),
            full((R_HEADS, c_len, c_len)),
            full((R_HEADS, c_len, R_DV)),
            full((R_HEADS, c_len, R_DK)),
            full((R_HEADS, c_len, R_DV)),
            full((R_HEADS, c_len, R_DK)),
        ],
        out_specs=[
            pl.BlockSpec((c_len, RET_V), lambda c: (c, 0)),
            pl.BlockSpec((c_len, RET_V), lambda c: (rev(c), 0)),
        ],
        out_shape=[jax.ShapeDtypeStruct((t, RET_V), F32), jax.ShapeDtypeStruct((t, RET_V), F32)],
        scratch_shapes=[pltpu.VMEM((R_HEADS, R_DK, R_DV), F32), pltpu.VMEM((R_HEADS, R_DK, R_DV), F32)],
        compiler_params=_cparams("arbitrary"),
        name="retention",
    )(dec, h, h, h, h, h, h, dmat, xif, ztf, xib, ztb)


ATT_TQ = 512
ATT_TK = 512
NEG_BIG = -1e30


def _attn_kernel(lam_ref, q_ref, k_ref, v_ref, gain_ref, o_ref, *, n_kv, post_scale):
    q1 = q_ref[:, :D_DH]
    q2 = q_ref[:, D_DH:]
    tq = q_ref.shape[0]

    def online(q, kk, vv, m, l, acc):
        s = lax.dot_general(q, kk, NT_DIMS, preferred_element_type=F32)
        m_new = jnp.maximum(m, jnp.max(s, axis=1, keepdims=True))
        alpha = jnp.exp2(m - m_new)
        p = jnp.exp2(s - m_new)
        l = alpha * l + jnp.sum(p, axis=1, keepdims=True)
        acc = alpha * acc + jnp.dot(p.astype(BF16), vv, preferred_element_type=F32)
        return m_new, l, acc

    def body(j, carry):
        m1, l1, a1, m2, l2, a2 = carry
        off = pl.multiple_of(j * ATT_TK, ATT_TK)
        kk = k_ref[pl.ds(off, ATT_TK), :]
        vv = v_ref[pl.ds(off, ATT_TK), :]
        m1, l1, a1 = online(q1, kk[:, :D_DH], vv, m1, l1, a1)
        m2, l2, a2 = online(q2, kk[:, D_DH:], vv, m2, l2, a2)
        return m1, l1, a1, m2, l2, a2

    m0 = jnp.full((tq, 1), NEG_BIG, F32)
    l0 = jnp.zeros((tq, 1), F32)
    a0 = jnp.zeros((tq, D_DV), F32)
    m1, l1, a1, m2, l2, a2 = lax.fori_loop(0, n_kv, body, (m0, l0, a0, m0, l0, a0))
    out = a1 / l1 - lam_ref[0] * (a2 / l2)
    ms = jnp.mean(out * out, axis=1, keepdims=True)
    out = out * lax.rsqrt(ms + EPS) * gain_ref[...] * post_scale
    o_ref[...] = out.astype(o_ref.dtype)


def _diff_attention(h, lam, gain, nseq, slen, row0, lam_init):
    assert row0 % slen == 0
    nq = slen // ATT_TQ
    qb0 = row0 // ATT_TQ
    sb0 = row0 // slen
    w = 2 * D_DH
    kern = functools.partial(_attn_kernel, n_kv=slen // ATT_TK, post_scale=1.0 - lam_init)
    return pl.pallas_call(
        kern,
        grid=(nseq, D_HEADS, nq),
        in_specs=[
            pl.BlockSpec(memory_space=pltpu.SMEM),
            pl.BlockSpec((ATT_TQ, w), lambda b, hd, i: (qb0 + b * nq + i, OFF_DQ // w + hd)),
            pl.BlockSpec((slen, w), lambda b, hd, i: (sb0 + b, OFF_DK // w + hd)),
            pl.BlockSpec((slen, D_DV), lambda b, hd, i: (sb0 + b, OFF_DV // D_DV + hd)),
            pl.BlockSpec((1, D_DV), lambda b, hd, i: (0, 0)),
        ],
        out_specs=pl.BlockSpec((ATT_TQ, D_DV), lambda b, hd, i: (b * nq + i, hd)),
        out_shape=jax.ShapeDtypeStruct((nseq * slen, DIFF_V), BF16),
        compiler_params=_cparams("parallel", "parallel", "parallel"),
        name="diff_attn_s%d" % slen,
    )(lam, h, h, h, gain)


MERGE_TM = 512


def _layer_norm(y, g, b):
    mu = jnp.mean(y, axis=1, keepdims=True)
    yc = y - mu
    var = jnp.mean(yc * yc, axis=1, keepdims=True)
    return yc * lax.rsqrt(var + EPS) * g + b


def _merge_kernel(x_ref, oa_ref, ob_ref, rg_ref, do_ref, ga_ref, gb_ref, gn_ref, wr_ref, wd_ref, wo_ref,
                  lng_ref, lnb_ref, rw_hi_ref, rw_lo_ref, x1_ref, x1b_ref, comb_ref):
    ro = oa_ref[...] + ob_ref[...]
    parts = []
    for hd in range(R_HEADS):
        seg = ro[:, hd * R_DV:(hd + 1) * R_DV]
        mu = jnp.mean(seg, axis=1, keepdims=True)
        sc = seg - mu
        var = jnp.mean(sc * sc, axis=1, keepdims=True)
        parts.append(sc * lax.rsqrt(var + EPS))
    ro = jnp.concatenate(parts, axis=1) * gn_ref[...]
    rg = rg_ref[...].astype(F32)
    ret_out = (rg * jax.nn.sigmoid(rg) * ro).astype(BF16)
    a = jnp.dot(ret_out, wr_ref[...], preferred_element_type=F32)
    b = jnp.dot(do_ref[...], wd_ref[...], preferred_element_type=F32)
    merged = jax.nn.sigmoid(ga_ref[...].astype(F32)) * a + jax.nn.sigmoid(gb_ref[...].astype(F32)) * b
    mix = jnp.dot(merged.astype(BF16), wo_ref[...], preferred_element_type=F32)
    x1 = _layer_norm(ALPHA * x_ref[...] + mix, lng_ref[...], lnb_ref[...])
    x1_ref[...] = x1
    x1_hi = x1.astype(BF16)
    x1b_ref[...] = x1_hi
    x1_lo = (x1 - x1_hi.astype(F32)).astype(BF16)
    logits = (jnp.dot(x1_hi, rw_hi_ref[...], preferred_element_type=F32)
              + jnp.dot(x1_hi, rw_lo_ref[...], preferred_element_type=F32)
              + jnp.dot(x1_lo, rw_hi_ref[...], preferred_element_type=F32))
    lane = lax.broadcasted_iota(jnp.int32, logits.shape, 1)
    is_g = lane < N_GROUPS
    lg = jnp.where(is_g, logits, NEG_BIG)
    gmax = jnp.max(lg, axis=1, keepdims=True)
    gsel = jnp.min(jnp.where(jnp.logical_and(is_g, lg == gmax), lane, LANES), axis=1, keepdims=True)
    gp = 1.0 / jnp.sum(jnp.where(is_g, jnp.exp(lg - gmax), 0.0), axis=1, keepdims=True)
    e_lane = lane - N_GROUPS
    in_grp = jnp.logical_and(jnp.logical_and(e_lane >= 0, e_lane < N_EXPERTS),
                             (e_lane >> 3) == gsel)
    le = jnp.where(in_grp, logits, NEG_BIG)
    m1 = jnp.max(le, axis=1, keepdims=True)
    i1 = jnp.min(jnp.where(jnp.logical_and(in_grp, le == m1), lane, LANES), axis=1, keepdims=True)
    le2 = jnp.where(lane == i1, NEG_BIG, le)
    m2 = jnp.max(le2, axis=1, keepdims=True)
    i2 = jnp.min(jnp.where(jnp.logical_and(in_grp, le2 == m2), lane, LANES), axis=1, keepdims=True)
    e2 = jnp.exp(m2 - m1)
    w1 = gp / (1.0 + e2)
    w2 = gp * e2 / (1.0 + e2)
    comb = jnp.where(lane == i1, w1, 0.0) + jnp.where(lane == i2, w2, 0.0)
    comb_ref[...] = comb


def _merge(x, oa, ob, h, dout, gn_gain, w_ret, w_diff, w_out, ln_g, ln_b, rw_hi, rw_lo):
    t = x.shape[0]
    tm = MERGE_TM
    row = lambda cb: pl.BlockSpec((tm, D_MODEL), lambda i: (i, cb))
    const = lambda shape: pl.BlockSpec(shape, lambda i: (0,) * len(shape))
    return pl.pallas_call(
        _merge_kernel,
        grid=(t // tm,),
        in_specs=[
            row(0), row(0), row(0),
            row(OFF_RG // D_MODEL),
            row(0),
            row(OFF_GA // D_MODEL),
            row(OFF_GB // D_MODEL),
            const((1, RET_V)),
            const((RET_V, D_MODEL)), const((DIFF_V, D_MODEL)), const((D_MODEL, D_MODEL)),
            const((1, D_MODEL)), const((1, D_MODEL)),
            const((D_MODEL, LANES)), const((D_MODEL, LANES)),
        ],
        out_specs=[row(0), row(0), pl.BlockSpec((tm, LANES), lambda i: (i, 0))],
        out_shape=[jax.ShapeDtypeStruct((t, D_MODEL), F32), jax.ShapeDtypeStruct((t, D_MODEL), BF16),
                   jax.ShapeDtypeStruct((t, LANES), F32)],
        compiler_params=_cparams("parallel"),
        name="merge",
    )(x, oa, ob, h, dout, h, h, gn_gain, w_ret, w_diff, w_out, ln_g, ln_b, rw_hi, rw_lo)


MOE_TM = 1024


def _moe_kernel(x_ref, xb_ref, comb_ref, wgu_ref, wdn_ref, lng_ref, lnb_ref, o_ref, acc_ref):
    e = pl.program_id(1)

    @pl.when(e == 0)
    def _():
        acc_ref[...] = jnp.zeros_like(acc_ref)

    hgu = jnp.dot(xb_ref[...], wgu_ref[0], preferred_element_type=F32)
    a = hgu[:, :D_FF_EXPERT]
    gate = hgu[:, D_FF_EXPERT:]
    act = (a * jax.nn.sigmoid(a) * gate).astype(BF16)
    y = jnp.dot(act, wdn_ref[0], preferred_element_type=F32)
    comb = comb_ref[...]
    lane = lax.broadcasted_iota(jnp.int32, comb.shape, 1)
    wcol = jnp.sum(jnp.where(lane == e + N_GROUPS, comb, 0.0), axis=1, keepdims=True)
    acc_ref[...] += wcol * y

    @pl.when(e == N_EXPERTS - 1)
    def _():
        o_ref[...] = _layer_norm(ALPHA * x_ref[...] + acc_ref[...], lng_ref[...], lnb_ref[...])


def _moe(x1, x1b, comb, w_gu, w_dn, ln_g, ln_b):
    t = x1.shape[0]
    tm = MOE_TM
    return pl.pallas_call(
        _moe_kernel,
        grid=(t // tm, N_EXPERTS),
        in_specs=[
            pl.BlockSpec((tm, D_MODEL), lambda i, e: (i, 0)),
            pl.BlockSpec((tm, D_MODEL), lambda i, e: (i, 0)),
            pl.BlockSpec((tm, LANES), lambda i, e: (i, 0)),
            pl.BlockSpec((1, D_MODEL, 2 * D_FF_EXPERT), lambda i, e: (e, 0, 0)),
            pl.BlockSpec((1, D_FF_EXPERT, D_MODEL), lambda i, e: (e, 0, 0)),
            pl.BlockSpec((1, D_MODEL), lambda i, e: (0, 0)),
            pl.BlockSpec((1, D_MODEL), lambda i, e: (0, 0)),
        ],
        out_specs=pl.BlockSpec((tm, D_MODEL), lambda i, e: (i, 0)),
        out_shape=jax.ShapeDtypeStruct((t, D_MODEL), F32),
        scratch_shapes=[pltpu.VMEM((tm, D_MODEL), F32)],
        compiler_params=_cparams("parallel", "arbitrary"),
        name="moe",
    )(x1, x1b, comb, w_gu, w_dn, ln_g, ln_b)


def _rope_tables(s):
    pos = jnp.arange(s, dtype=F32)
    inv = ROPE_THETA ** (-jnp.arange(0, R_DK, 2, dtype=F32) / R_DK)
    ang = pos[:, None] * inv[None, :]
    cos, sin = jnp.cos(ang), jnp.sin(ang)
    return jnp.concatenate([cos, cos], axis=1), jnp.concatenate([-sin, sin], axis=1)


def kernel(x_prompt, x_sample, w_in, ret_log_decay, ret_gn_gain, diff_lambda, diff_subln_gain, w_ret_branch,
           w_diff_branch, w_out, ln1_g, ln1_b, router_group, router_expert, w_gate_up, w_down, ln2_g, ln2_b):
    bp, sp, _ = x_prompt.shape
    bs, ss, _ = x_sample.shape
    tp, ts = bp * sp, bs * ss
    assert sp % PROJ_TM == 0 and ss % PROJ_TM == 0 and sp % ATT_TQ == 0 and ss % ATT_TQ == 0
    x = jnp.concatenate([x_prompt.reshape(tp, D_MODEL), x_sample.reshape(ts, D_MODEL)], axis=0)

    cos_t, sin_t = _rope_tables(max(sp, ss))
    n_p, per_p, per_s = tp // PROJ_TM, sp // PROJ_TM, ss // PROJ_TM

    def pos_block(i):
        return jnp.where(i < n_p, i % per_p, (i - n_p) % per_s)

    colscale = jnp.ones((IN_WIDTH,), F32)
    colscale = colscale.at[OFF_RK:OFF_RK + RET_QK].set(R_DK ** -0.5)
    colscale = colscale.at[OFF_DQ:OFF_DQ + DIFF_QK].set(math.log2(math.e) / math.sqrt(D_DH))
    colscale = colscale.reshape(1, IN_WIDTH)

    xb = x.astype(BF16)
    for l in range(DEPTH):
        h = _inproj(xb, w_in[l].astype(BF16), cos_t, sin_t, colscale, pos_block)
        oa, ob = _retention(h, ret_log_decay[l], ((bp, sp), (bs, ss)))
        lam_init = 0.8 - 0.6 * math.exp(-0.3 * l)
        lf = diff_lambda[l].astype(F32)
        lam = (jnp.exp(jnp.sum(lf[0] * lf[1])) - jnp.exp(jnp.sum(lf[2] * lf[3])) + lam_init).reshape(1)
        gain = diff_subln_gain[l].astype(F32).reshape(1, D_DV)
        dout = jnp.concatenate([
            _diff_attention(h, lam, gain, bp, sp, 0, lam_init),
            _diff_attention(h, lam, gain, bs, ss, tp, lam_init),
        ], axis=0)
        rw = jnp.concatenate([router_group[l], router_expert[l],
                              jnp.zeros((D_MODEL, LANES - N_GROUPS - N_EXPERTS), F32)], axis=1).astype(F32)
        rw_hi = rw.astype(BF16)
        rw_lo = (rw - rw_hi.astype(F32)).astype(BF16)
        x1, x1b, comb = _merge(x, oa, ob, h, dout, ret_gn_gain[l].astype(F32).reshape(1, RET_V),
                               w_ret_branch[l].astype(BF16), w_diff_branch[l].astype(BF16),
                               w_out[l].astype(BF16), ln1_g[l].reshape(1, D_MODEL), ln1_b[l].reshape(1, D_MODEL),
                               rw_hi, rw_lo)
        x = _moe(x1, x1b, comb, w_gate_up[l].astype(BF16), w_down[l].astype(BF16),
                 ln2_g[l].reshape(1, D_MODEL), ln2_b[l].reshape(1, D_MODEL))
        xb = x.astype(BF16)
    return x[:tp].reshape(bp, sp, D_MODEL), x[tp:].reshape(bs, ss, D_MODEL)
```

```python
import functools
import math

import jax
import jax.numpy as jnp
from jax import lax
from jax.experimental import pallas as pl
from jax.experimental.pallas import tpu as pltpu

F32 = jnp.float32
BF16 = jnp.bfloat16

D_MODEL = 1024
DEPTH = 2
R_HEADS = 4
R_DK = 128
R_DV = 256
D_HEADS = 4
D_DH = 128
D_DV = 256
RET_QK = R_HEADS * R_DK
RET_V = R_HEADS * R_DV
DIFF_QK = D_HEADS * 2 * D_DH
DIFF_V = D_HEADS * D_DV
IN_WIDTH = 2 * RET_QK + 2 * RET_V + 2 * DIFF_QK + DIFF_V + 2 * D_MODEL
OFF_RQ = 0
OFF_RK = OFF_RQ + RET_QK
OFF_RV = OFF_RK + RET_QK
OFF_RG = OFF_RV + RET_V
OFF_DQ = OFF_RG + RET_V
OFF_DK = OFF_DQ + DIFF_QK
OFF_DV = OFF_DK + DIFF_QK
OFF_GA = OFF_DV + DIFF_V
OFF_GB = OFF_GA + D_MODEL
ROPE_THETA = 10000.0
N_GROUPS = 4
EXPERTS_PER_GROUP = 8
N_EXPERTS = N_GROUPS * EXPERTS_PER_GROUP
D_FF_EXPERT = 512
ALPHA = (2.0 * DEPTH) ** 0.25
EPS = 1e-5

LANES = 128
RET_CHUNK = 128
VMEM_LIMIT = 56 * 1024 * 1024

NT_DIMS = (((1,), (1,)), ((), ()))
TN_DIMS = (((0,), (0,)), ((), ()))


def _cparams(*sem):
    return pltpu.CompilerParams(dimension_semantics=sem, vmem_limit_bytes=VMEM_LIMIT)


PROJ_TM = 1024
PROJ_TN = 1024
ROPE_TILES = tuple(sorted({OFF_RQ // PROJ_TN, OFF_RK // PROJ_TN, OFF_DQ // PROJ_TN, OFF_DK // PROJ_TN,
                           (OFF_DK + DIFF_QK - 1) // PROJ_TN}))


PROJ_SUB = 256


def _inproj_kernel(x_ref, w_ref, cos_ref, sin_ref, scale_ref, o_ref):
    j = pl.program_id(0)
    is_rope = functools.reduce(jnp.logical_or, [j == t for t in ROPE_TILES])

    @pl.when(is_rope)
    def _():
        cos = cos_ref[...]
        sin = sin_ref[...]
        for sb in range(PROJ_TN // PROJ_SUB):
            acc = jnp.dot(x_ref[...], w_ref[:, sb * PROJ_SUB:(sb + 1) * PROJ_SUB], preferred_element_type=F32)
            for g in range(PROJ_SUB // LANES):
                sl = slice(sb * PROJ_SUB + g * LANES, sb * PROJ_SUB + (g + 1) * LANES)
                a = acc[:, g * LANES:(g + 1) * LANES]
                r = pltpu.roll(a, LANES // 2, 1)
                o_ref[:, sl] = ((a * cos + r * sin) * scale_ref[:, sl]).astype(o_ref.dtype)

    @pl.when(jnp.logical_not(is_rope))
    def _():
        for sb in range(PROJ_TN // PROJ_SUB):
            sl = slice(sb * PROJ_SUB, (sb + 1) * PROJ_SUB)
            o_ref[:, sl] = jnp.dot(x_ref[...], w_ref[:, sl], preferred_element_type=F32).astype(o_ref.dtype)


def _inproj(xb, w, cos_t, sin_t, colscale, pos_block):
    t = xb.shape[0]
    grid = (IN_WIDTH // PROJ_TN, t // PROJ_TM)
    return pl.pallas_call(
        _inproj_kernel,
        grid=grid,
        in_specs=[
            pl.BlockSpec((PROJ_TM, D_MODEL), lambda j, i: (i, 0)),
            pl.BlockSpec((D_MODEL, PROJ_TN), lambda j, i: (0, j)),
            pl.BlockSpec((PROJ_TM, LANES), lambda j, i: (pos_block(i), 0)),
            pl.BlockSpec((PROJ_TM, LANES), lambda j, i: (pos_block(i), 0)),
            pl.BlockSpec((1, PROJ_TN), lambda j, i: (0, j)),
        ],
        out_specs=pl.BlockSpec((PROJ_TM, PROJ_TN), lambda j, i: (i, j)),
        out_shape=jax.ShapeDtypeStruct((t, IN_WIDTH), BF16),
        compiler_params=_cparams("parallel", "parallel"),
        name="inproj",
    )(xb, w, cos_t, sin_t, colscale)


def _ret_kernel(dec_ref, qf_ref, kf_ref, vf_ref, qb_ref, kb_ref, vb_ref, dmat_ref, xif_ref, ztf_ref,
                xib_ref, ztb_ref, oa_ref, ob_ref, sf_ref, sb_ref, *, is_first):
    c = pl.program_id(0)

    @pl.when(is_first(c))
    def _():
        sf_ref[...] = jnp.zeros_like(sf_ref)
        sb_ref[...] = jnp.zeros_like(sb_ref)

    for hd in range(R_HEADS):
        qs = slice(hd * R_DK, (hd + 1) * R_DK)
        vs = slice(hd * R_DV, (hd + 1) * R_DV)
        q = qf_ref[:, qs]
        k = kf_ref[:, qs]
        v = vf_ref[:, vs]
        s = lax.dot_general(q, k, NT_DIMS, preferred_element_type=F32)
        p = (s * dmat_ref[hd]).astype(BF16)
        o = jnp.dot(p, v, preferred_element_type=F32)
        st = sf_ref[hd]
        o = o + xif_ref[hd] * jnp.dot(q, st.astype(BF16), preferred_element_type=F32)
        kz = (k.astype(F32) * ztf_ref[hd]).astype(BF16)
        sf_ref[hd] = st * dec_ref[0, hd] + lax.dot_general(kz, v, TN_DIMS, preferred_element_type=F32)
        oa_ref[:, vs] = o
        q = qb_ref[:, qs]
        k = kb_ref[:, qs]
        v = vb_ref[:, vs]
        st = sb_ref[hd]
        ob_ref[:, vs] = xib_ref[hd] * jnp.dot(q, st.astype(BF16), preferred_element_type=F32)
        kz = (k.astype(F32) * ztb_ref[hd]).astype(BF16)
        sb_ref[hd] = st * dec_ref[1, hd] + lax.dot_general(kz, v, TN_DIMS, preferred_element_type=F32)


def _retention(h, log_decay, seqs):
    t = h.shape[0]
    c_len = RET_CHUNK
    n_chunks = t // c_len
    bounds = []
    start = 0
    for nseq, slen in seqs:
        per = slen // c_len
        bounds.append((start, start + nseq * per, per))
        start += nseq * per

    def is_first(c):
        out = False
        for lo, hi, per in bounds:
            out = jnp.logical_or(out, jnp.logical_and(jnp.logical_and(c >= lo, c < hi), (c - lo) % per == 0))
        return out

    def rev(c):
        out = c
        for lo, hi, per in bounds:
            r = lo + ((c - lo) // per) * per + (per - 1 - (c - lo) % per)
            out = jnp.where(jnp.logical_and(c >= lo, c < hi), r, out)
        return out

    lf = -jnp.abs(log_decay[0].astype(F32))
    lb = -jnp.abs(log_decay[1].astype(F32))
    idx = jnp.arange(c_len, dtype=F32)
    diff = idx[:, None] - idx[None, :]
    dmat = jnp.where(diff[None] >= 0,
                     jnp.exp(lf[:, None, None] * jnp.maximum(diff, 0.0)[None]),
                     jnp.exp(lb[:, None, None] * jnp.maximum(-diff, 0.0)[None]))
    ones_v = jnp.ones((1, 1, R_DV), F32)
    ones_k = jnp.ones((1, 1, R_DK), F32)
    xif = jnp.exp(lf[:, None, None] * (idx[None, :, None] + 1.0)) * ones_v
    ztf = jnp.exp(lf[:, None, None] * (c_len - 1.0 - idx[None, :, None])) * ones_k
    xib = jnp.exp(lb[:, None, None] * (c_len - idx[None, :, None])) * ones_v
    ztb = jnp.exp(lb[:, None, None] * idx[None, :, None]) * ones_k
    dec = jnp.stack([jnp.exp(lf * c_len), jnp.exp(lb * c_len)])

    qk_w = RET_QK
    full = lambda shape: pl.BlockSpec(shape, lambda c: (0,) * len(shape))
    kern = functools.partial(_ret_kernel, is_first=is_first)
    return pl.pallas_call(
        kern,
        grid=(n_chunks,),
        in_specs=[
            pl.BlockSpec(memory_space=pltpu.SMEM),
            pl.BlockSpec((c_len, qk_w), lambda c: (c, OFF_RQ // qk_w)),
            pl.BlockSpec((c_len, qk_w), lambda c: (c, OFF_RK // qk_w)),
            pl.BlockSpec((c_len, RET_V), lambda c: (c, OFF_RV // RET_V)),
            pl.BlockSpec((c_len, qk_w), lambda c: (rev(c), OFF_RQ // qk_w)),
            pl.BlockSpec((c_len, qk_w), lambda c: (rev(c), OFF_RK // qk_w)),
            pl.BlockSpec((c_len, RET_V), lambda c: (rev(c), OFF_RV // RET_V)),
            full((R_HEADS, c_len, c_len)),
            full((R_HEADS, c_len, R_DV)),
            full((R_HEADS, c_len, R_DK)),
            full((R_HEADS, c_len, R_DV)),
            full((R_HEADS, c_len, R_DK)),
        ],
        out_specs=[
            pl.BlockSpec((c_len, RET_V), lambda c: (c, 0)),
            pl.BlockSpec((c_len, RET_V), lambda c: (rev(c), 0)),
        ],
        out_shape=[jax.ShapeDtypeStruct((t, RET_V), F32), jax.ShapeDtypeStruct((t, RET_V), F32)],
        scratch_shapes=[pltpu.VMEM((R_HEADS, R_DK, R_DV), F32), pltpu.VMEM((R_HEADS, R_DK, R_DV), F32)],
        compiler_params=_cparams("arbitrary"),
        name="retention",
    )(dec, h, h, h, h, h, h, dmat, xif, ztf, xib, ztb)


ATT_TQ = 512
ATT_TK = 1024
ATT_STRIP = 32
NEG_BIG = -1e30


def _attn_kernel(lam_ref, q_ref, k_ref, v_ref, gain_ref, o_ref, s_ref, p_ref, m_ref, l_ref, al_ref, acc_ref,
                 *, tk, n_kv, post_scale):
    tq = q_ref.shape[0]
    n_col = tk // LANES
    m_ref[...] = jnp.full(m_ref.shape, NEG_BIG, F32)
    l_ref[...] = jnp.zeros(l_ref.shape, F32)
    acc_ref[...] = jnp.zeros(acc_ref.shape, F32)

    def body(j, carry):
        off = pl.multiple_of(j * tk, tk)
        vv = v_ref[pl.ds(off, tk), :]
        for mp in range(2):
            q = q_ref[:, mp * D_DH:(mp + 1) * D_DH]
            kk = k_ref[pl.ds(off, tk), mp * D_DH:(mp + 1) * D_DH]
            s_ref[mp] = lax.dot_general(q, kk, NT_DIMS, preferred_element_type=F32)
        for mp in range(2):
            for r in range(tq // ATT_STRIP):
                rows = slice(r * ATT_STRIP, (r + 1) * ATT_STRIP)
                cols = [s_ref[mp, rows, c * LANES:(c + 1) * LANES] for c in range(n_col)]
                smax = functools.reduce(jnp.maximum, cols)
                m_old = m_ref[mp, rows, :]
                m_new = jnp.maximum(m_old, jnp.max(smax, axis=1, keepdims=True))
                alpha = jnp.exp2(m_old - m_new)
                lsum = None
                for c in range(n_col):
                    p = jnp.exp2(cols[c] - m_new)
                    p_ref[mp, rows, c * LANES:(c + 1) * LANES] = p.astype(BF16)
                    lsum = p if lsum is None else lsum + p
                l_ref[mp, rows, :] = alpha * l_ref[mp, rows, :] + lsum
                m_ref[mp, rows, :] = m_new
                al_ref[mp, rows, :] = alpha
        for mp in range(2):
            pv = jnp.dot(p_ref[mp], vv, preferred_element_type=F32)
            al = al_ref[mp]
            acc_ref[mp] = jnp.concatenate([al] * (D_DV // LANES), axis=1) * acc_ref[mp] + pv
        return carry

    lax.fori_loop(0, n_kv, body, 0)
    l1 = jnp.sum(l_ref[0], axis=1, keepdims=True)
    l2 = jnp.sum(l_ref[1], axis=1, keepdims=True)
    out = acc_ref[0] / l1 - lam_ref[0] * (acc_ref[1] / l2)
    ms = jnp.mean(out * out, axis=1, keepdims=True)
    out = out * lax.rsqrt(ms + EPS) * gain_ref[...] * post_scale
    o_ref[...] = out.astype(o_ref.dtype)


def _diff_attention(h, lam, gain, nseq, slen, row0, lam_init):
    assert row0 % slen == 0
    tq, tk = ATT_TQ, min(ATT_TK, slen)
    nq = slen // tq
    qb0 = row0 // tq
    sb0 = row0 // slen
    w = 2 * D_DH
    kern = functools.partial(_attn_kernel, tk=tk, n_kv=slen // tk, post_scale=1.0 - lam_init)
    return pl.pallas_call(
        kern,
        grid=(nseq, D_HEADS, nq),
        in_specs=[
            pl.BlockSpec(memory_space=pltpu.SMEM),
            pl.BlockSpec((tq, w), lambda b, hd, i: (qb0 + b * nq + i, OFF_DQ // w + hd)),
            pl.BlockSpec((slen, w), lambda b, hd, i: (sb0 + b, OFF_DK // w + hd)),
            pl.BlockSpec((slen, D_DV), lambda b, hd, i: (sb0 + b, OFF_DV // D_DV + hd)),
            pl.BlockSpec((1, D_DV), lambda b, hd, i: (0, 0)),
        ],
        out_specs=pl.BlockSpec((tq, D_DV), lambda b, hd, i: (b * nq + i, hd)),
        out_shape=jax.ShapeDtypeStruct((nseq * slen, DIFF_V), BF16),
        scratch_shapes=[
            pltpu.VMEM((2, tq, tk), F32), pltpu.VMEM((2, tq, tk), BF16),
            pltpu.VMEM((2, tq, LANES), F32), pltpu.VMEM((2, tq, LANES), F32), pltpu.VMEM((2, tq, LANES), F32),
            pltpu.VMEM((2, tq, D_DV), F32),
        ],
        compiler_params=_cparams("parallel", "parallel", "parallel"),
        name="diff_attn_s%d" % slen,
    )(lam, h, h, h, gain)


MERGE_TM = 512


def _layer_norm(y, g, b):
    mu = jnp.mean(y, axis=1, keepdims=True)
    yc = y - mu
    var = jnp.mean(yc * yc, axis=1, keepdims=True)
    return yc * lax.rsqrt(var + EPS) * g + b


def _merge_kernel(x_ref, oa_ref, ob_ref, rg_ref, do_ref, ga_ref, gb_ref, gn_ref, wr_ref, wd_ref, wo_ref,
                  lng_ref, lnb_ref, rw_hi_ref, rw_lo_ref, x1_ref, x1b_ref, comb_ref):
    ro = oa_ref[...] + ob_ref[...]
    parts = []
    for hd in range(R_HEADS):
        seg = ro[:, hd * R_DV:(hd + 1) * R_DV]
        mu = jnp.mean(seg, axis=1, keepdims=True)
        sc = seg - mu
        var = jnp.mean(sc * sc, axis=1, keepdims=True)
        parts.append(sc * lax.rsqrt(var + EPS))
    ro = jnp.concatenate(parts, axis=1) * gn_ref[...]
    rg = rg_ref[...].astype(F32)
    ret_out = (rg * jax.nn.sigmoid(rg) * ro).astype(BF16)
    a = jnp.dot(ret_out, wr_ref[...], preferred_element_type=F32)
    b = jnp.dot(do_ref[...], wd_ref[...], preferred_element_type=F32)
    merged = jax.nn.sigmoid(ga_ref[...].astype(F32)) * a + jax.nn.sigmoid(gb_ref[...].astype(F32)) * b
    mix = jnp.dot(merged.astype(BF16), wo_ref[...], preferred_element_type=F32)
    x1 = _layer_norm(ALPHA * x_ref[...] + mix, lng_ref[...], lnb_ref[...])
    x1_ref[...] = x1
    x1_hi = x1.astype(BF16)
    x1b_ref[...] = x1_hi
    x1_lo = (x1 - x1_hi.astype(F32)).astype(BF16)
    logits = (jnp.dot(x1_hi, rw_hi_ref[...], preferred_element_type=F32)
              + jnp.dot(x1_hi, rw_lo_ref[...], preferred_element_type=F32)
              + jnp.dot(x1_lo, rw_hi_ref[...], preferred_element_type=F32))
    lane = lax.broadcasted_iota(jnp.int32, logits.shape, 1)
    is_g = lane < N_GROUPS
    lg = jnp.where(is_g, logits, NEG_BIG)
    gmax = jnp.max(lg, axis=1, keepdims=True)
    gsel = jnp.min(jnp.where(jnp.logical_and(is_g, lg == gmax), lane, LANES), axis=1, keepdims=True)
    gp = 1.0 / jnp.sum(jnp.where(is_g, jnp.exp(lg - gmax), 0.0), axis=1, keepdims=True)
    e_lane = lane - N_GROUPS
    in_grp = jnp.logical_and(jnp.logical_and(e_lane >= 0, e_lane < N_EXPERTS),
                             (e_lane >> 3) == gsel)
    le = jnp.where(in_grp, logits, NEG_BIG)
    m1 = jnp.max(le, axis=1, keepdims=True)
    i1 = jnp.min(jnp.where(jnp.logical_and(in_grp, le == m1), lane, LANES), axis=1, keepdims=True)
    le2 = jnp.where(lane == i1, NEG_BIG, le)
    m2 = jnp.max(le2, axis=1, keepdims=True)
    i2 = jnp.min(jnp.where(jnp.logical_and(in_grp, le2 == m2), lane, LANES), axis=1, keepdims=True)
    e2 = jnp.exp(m2 - m1)
    w1 = gp / (1.0 + e2)
    w2 = gp * e2 / (1.0 + e2)
    comb = jnp.where(lane == i1, w1, 0.0) + jnp.where(lane == i2, w2, 0.0)
    comb_ref[...] = comb


def _merge(x, oa, ob, h, dout, gn_gain, w_ret, w_diff, w_out, ln_g, ln_b, rw_hi, rw_lo):
    t = x.shape[0]
    tm = MERGE_TM
    row = lambda cb: pl.BlockSpec((tm, D_MODEL), lambda i: (i, cb))
    const = lambda shape: pl.BlockSpec(shape, lambda i: (0,) * len(shape))
    return pl.pallas_call(
        _merge_kernel,
        grid=(t // tm,),
        in_specs=[
            row(0), row(0), row(0),
            row(OFF_RG // D_MODEL),
            row(0),
            row(OFF_GA // D_MODEL),
            row(OFF_GB // D_MODEL),
            const((1, RET_V)),
            const((RET_V, D_MODEL)), const((DIFF_V, D_MODEL)), const((D_MODEL, D_MODEL)),
            const((1, D_MODEL)), const((1, D_MODEL)),
            const((D_MODEL, LANES)), const((D_MODEL, LANES)),
        ],
        out_specs=[row(0), row(0), pl.BlockSpec((tm, LANES), lambda i: (i, 0))],
        out_shape=[jax.ShapeDtypeStruct((t, D_MODEL), F32), jax.ShapeDtypeStruct((t, D_MODEL), BF16),
                   jax.ShapeDtypeStruct((t, LANES), F32)],
        compiler_params=_cparams("parallel"),
        name="merge",
    )(x, oa, ob, h, dout, h, h, gn_gain, w_ret, w_diff, w_out, ln_g, ln_b, rw_hi, rw_lo)


MOE_TM = 1024


def _moe_kernel(x_ref, xb_ref, comb_ref, wgu_ref, wdn_ref, lng_ref, lnb_ref, o_ref, acc_ref):
    e = pl.program_id(1)

    @pl.when(e == 0)
    def _():
        acc_ref[...] = jnp.zeros_like(acc_ref)

    hgu = jnp.dot(xb_ref[...], wgu_ref[0], preferred_element_type=F32)
    a = hgu[:, :D_FF_EXPERT]
    gate = hgu[:, D_FF_EXPERT:]
    act = (a * jax.nn.sigmoid(a) * gate).astype(BF16)
    y = jnp.dot(act, wdn_ref[0], preferred_element_type=F32)
    comb = comb_ref[...]
    lane = lax.broadcasted_iota(jnp.int32, comb.shape, 1)
    wcol = jnp.sum(jnp.where(lane == e + N_GROUPS, comb, 0.0), axis=1, keepdims=True)
    acc_ref[...] += wcol * y

    @pl.when(e == N_EXPERTS - 1)
    def _():
        o_ref[...] = _layer_norm(ALPHA * x_ref[...] + acc_ref[...], lng_ref[...], lnb_ref[...])


def _moe(x1, x1b, comb, w_gu, w_dn, ln_g, ln_b):
    t = x1.shape[0]
    tm = MOE_TM
    return pl.pallas_call(
        _moe_kernel,
        grid=(t // tm, N_EXPERTS),
        in_specs=[
            pl.BlockSpec((tm, D_MODEL), lambda i, e: (i, 0)),
            pl.BlockSpec((tm, D_MODEL), lambda i, e: (i, 0)),
            pl.BlockSpec((tm, LANES), lambda i, e: (i, 0)),
            pl.BlockSpec((1, D_MODEL, 2 * D_FF_EXPERT), lambda i, e: (e, 0, 0)),
            pl.BlockSpec((1, D_FF_EXPERT, D_MODEL), lambda i, e: (e, 0, 0)),
            pl.BlockSpec((1, D_MODEL), lambda i, e: (0, 0)),
            pl.BlockSpec((1, D_MODEL), lambda i, e: (0, 0)),
        ],
        out_specs=pl.BlockSpec((tm, D_MODEL), lambda i, e: (i, 0)),
        out_shape=jax.ShapeDtypeStruct((t, D_MODEL), F32),
        scratch_shapes=[pltpu.VMEM((tm, D_MODEL), F32)],
        compiler_params=_cparams("parallel", "arbitrary"),
        name="moe",
    )(x1, x1b, comb, w_gu, w_dn, ln_g, ln_b)


def _rope_tables(s):
    pos = jnp.arange(s, dtype=F32)
    inv = ROPE_THETA ** (-jnp.arange(0, R_DK, 2, dtype=F32) / R_DK)
    ang = pos[:, None] * inv[None, :]
    cos, sin = jnp.cos(ang), jnp.sin(ang)
    return jnp.concatenate([cos, cos], axis=1), jnp.concatenate([-sin, sin], axis=1)


def kernel(x_prompt, x_sample, w_in, ret_log_decay, ret_gn_gain, diff_lambda, diff_subln_gain, w_ret_branch,
           w_diff_branch, w_out, ln1_g, ln1_b, router_group, router_expert, w_gate_up, w_down, ln2_g, ln2_b):
    bp, sp, _ = x_prompt.shape
    bs, ss, _ = x_sample.shape
    tp, ts = bp * sp, bs * ss
    assert sp % PROJ_TM == 0 and ss % PROJ_TM == 0 and sp % ATT_TQ == 0 and ss % ATT_TQ == 0
    x = jnp.concatenate([x_prompt.reshape(tp, D_MODEL), x_sample.reshape(ts, D_MODEL)], axis=0)

    cos_t, sin_t = _rope_tables(max(sp, ss))
    n_p, per_p, per_s = tp // PROJ_TM, sp // PROJ_TM, ss // PROJ_TM

    def pos_block(i):
        return jnp.where(i < n_p, i % per_p, (i - n_p) % per_s)

    colscale = jnp.ones((IN_WIDTH,), F32)
    colscale = colscale.at[OFF_RK:OFF_RK + RET_QK].set(R_DK ** -0.5)
    colscale = colscale.at[OFF_DQ:OFF_DQ + DIFF_QK].set(math.log2(math.e) / math.sqrt(D_DH))
    colscale = colscale.reshape(1, IN_WIDTH)

    xb = x.astype(BF16)
    for l in range(DEPTH):
        h = _inproj(xb, w_in[l].astype(BF16), cos_t, sin_t, colscale, pos_block)
        oa, ob = _retention(h, ret_log_decay[l], ((bp, sp), (bs, ss)))
        lam_init = 0.8 - 0.6 * math.exp(-0.3 * l)
        lf = diff_lambda[l].astype(F32)
        lam = (jnp.exp(jnp.sum(lf[0] * lf[1])) - jnp.exp(jnp.sum(lf[2] * lf[3])) + lam_init).reshape(1)
        gain = diff_subln_gain[l].astype(F32).reshape(1, D_DV)
        dout = jnp.concatenate([
            _diff_attention(h, lam, gain, bp, sp, 0, lam_init),
            _diff_attention(h, lam, gain, bs, ss, tp, lam_init),
        ], axis=0)
        rw = jnp.concatenate([router_group[l], router_expert[l],
                              jnp.zeros((D_MODEL, LANES - N_GROUPS - N_EXPERTS), F32)], axis=1).astype(F32)
        rw_hi = rw.astype(BF16)
        rw_lo = (rw - rw_hi.astype(F32)).astype(BF16)
        x1, x1b, comb = _merge(x, oa, ob, h, dout, ret_gn_gain[l].astype(F32).reshape(1, RET_V),
                               w_ret_branch[l].astype(BF16), w_diff_branch[l].astype(BF16),
                               w_out[l].astype(BF16), ln1_g[l].reshape(1, D_MODEL), ln1_b[l].reshape(1, D_MODEL),
                               rw_hi, rw_lo)
        x = _moe(x1, x1b, comb, w_gate_up[l].astype(BF16), w_down[l].astype(BF16),
                 ln2_g[l].reshape(1, D_MODEL), ln2_b[l].reshape(1, D_MODEL))
        xb = x.astype(BF16)
    return x[:tp].reshape(bp, sp, D_MODEL), x[tp:].reshape(bs, ss, D_MODEL)
```

```python
import functools
import math

import jax
import jax.numpy as jnp
from jax import lax
from jax.experimental import pallas as pl
from jax.experimental.pallas import tpu as pltpu
from jax.experimental.pallas import tpu_sc as plsc

F32 = jnp.float32
BF16 = jnp.bfloat16

D_MODEL = 1024
DEPTH = 2
R_HEADS = 4
R_DK = 128
R_DV = 256
D_HEADS = 4
D_DH = 128
D_DV = 256
RET_QK = R_HEADS * R_DK
RET_V = R_HEADS * R_DV
DIFF_QK = D_HEADS * 2 * D_DH
DIFF_V = D_HEADS * D_DV
IN_WIDTH = 2 * RET_QK + 2 * RET_V + 2 * DIFF_QK + DIFF_V + 2 * D_MODEL
OFF_RQ = 0
OFF_RK = OFF_RQ + RET_QK
OFF_RV = OFF_RK + RET_QK
OFF_RG = OFF_RV + RET_V
OFF_DQ = OFF_RG + RET_V
OFF_DK = OFF_DQ + DIFF_QK
OFF_DV = OFF_DK + DIFF_QK
OFF_GA = OFF_DV + DIFF_V
OFF_GB = OFF_GA + D_MODEL
ROPE_THETA = 10000.0
N_GROUPS = 4
EXPERTS_PER_GROUP = 8
N_EXPERTS = N_GROUPS * EXPERTS_PER_GROUP
D_FF_EXPERT = 512
ALPHA = (2.0 * DEPTH) ** 0.25
EPS = 1e-5

LANES = 128
RET_CHUNK = 128
VMEM_LIMIT = 56 * 1024 * 1024

NT_DIMS = (((1,), (1,)), ((), ()))
TN_DIMS = (((0,), (0,)), ((), ()))


def _cparams(*sem):
    return pltpu.CompilerParams(dimension_semantics=sem, vmem_limit_bytes=VMEM_LIMIT)


PROJ_TM = 1024
PROJ_TN = 1024
assert DIFF_QK == PROJ_TN and OFF_DK % PROJ_TN == 0
DK_TILE = OFF_DK // PROJ_TN
ROPE_TILES = tuple(sorted({OFF_RQ // PROJ_TN, OFF_RK // PROJ_TN, OFF_DQ // PROJ_TN}))
PROJ_SUB = 256


def _proj_col_tile(j):
    return j + (j >= DK_TILE).astype(jnp.int32)


def _h_off(off):
    assert off < OFF_DK or off >= OFF_DK + DIFF_QK
    return off if off < OFF_DK else off - DIFF_QK


def _inproj_kernel(x_ref, w_ref, cos_ref, sin_ref, scale_ref, o_ref):
    ct = _proj_col_tile(pl.program_id(0))
    is_rope = functools.reduce(jnp.logical_or, [ct == t for t in ROPE_TILES])

    @pl.when(is_rope)
    def _():
        cos = cos_ref[...]
        sin = sin_ref[...]
        for sb in range(PROJ_TN // PROJ_SUB):
            acc = jnp.dot(x_ref[...], w_ref[:, sb * PROJ_SUB:(sb + 1) * PROJ_SUB], preferred_element_type=F32)
            for g in range(PROJ_SUB // LANES):
                sl = slice(sb * PROJ_SUB + g * LANES, sb * PROJ_SUB + (g + 1) * LANES)
                a = acc[:, g * LANES:(g + 1) * LANES]
                r = pltpu.roll(a, LANES // 2, 1)
                o_ref[:, sl] = ((a * cos + r * sin) * scale_ref[:, sl]).astype(o_ref.dtype)

    @pl.when(jnp.logical_not(is_rope))
    def _():
        for sb in range(PROJ_TN // PROJ_SUB):
            sl = slice(sb * PROJ_SUB, (sb + 1) * PROJ_SUB)
            o_ref[:, sl] = jnp.dot(x_ref[...], w_ref[:, sl], preferred_element_type=F32).astype(o_ref.dtype)


def _inproj(xb, w, cos_t, sin_t, colscale, pos_block):
    t = xb.shape[0]
    grid = (IN_WIDTH // PROJ_TN - 1, t // PROJ_TM)
    return pl.pallas_call(
        _inproj_kernel,
        grid=grid,
        in_specs=[
            pl.BlockSpec((PROJ_TM, D_MODEL), lambda j, i: (i, 0)),
            pl.BlockSpec((D_MODEL, PROJ_TN), lambda j, i: (0, _proj_col_tile(j))),
            pl.BlockSpec((PROJ_TM, LANES), lambda j, i: (pos_block(i), 0)),
            pl.BlockSpec((PROJ_TM, LANES), lambda j, i: (pos_block(i), 0)),
            pl.BlockSpec((1, PROJ_TN), lambda j, i: (0, _proj_col_tile(j))),
        ],
        out_specs=pl.BlockSpec((PROJ_TM, PROJ_TN), lambda j, i: (i, j)),
        out_shape=jax.ShapeDtypeStruct((t, IN_WIDTH - DIFF_QK), BF16),
        compiler_params=_cparams("parallel", "parallel"),
        name="inproj",
    )(xb, w, cos_t, sin_t, colscale)


def _kproj_kernel(x_ref, w_ref, cos_ref, sin_ref, o_ref):
    cos = cos_ref[...]
    sin = sin_ref[...]
    for sb in range(PROJ_TN // PROJ_SUB):
        acc = jnp.dot(x_ref[...], w_ref[:, sb * PROJ_SUB:(sb + 1) * PROJ_SUB], preferred_element_type=F32)
        for g in range(PROJ_SUB // LANES):
            a = acc[:, g * LANES:(g + 1) * LANES]
            r = pltpu.roll(a, LANES // 2, 1)
            row0 = sb * PROJ_SUB + g * LANES
            o_ref[0, row0:row0 + LANES, :] = jnp.transpose(a * cos + r * sin).astype(o_ref.dtype)


def _kproj(xb, w, cos_t, sin_t, pos_block):
    t = xb.shape[0]
    return pl.pallas_call(
        _kproj_kernel,
        grid=(t // PROJ_TM,),
        in_specs=[
            pl.BlockSpec((PROJ_TM, D_MODEL), lambda i: (i, 0)),
            pl.BlockSpec((D_MODEL, PROJ_TN), lambda i: (0, DK_TILE)),
            pl.BlockSpec((PROJ_TM, LANES), lambda i: (pos_block(i), 0)),
            pl.BlockSpec((PROJ_TM, LANES), lambda i: (pos_block(i), 0)),
        ],
        out_specs=pl.BlockSpec((1, DIFF_QK, PROJ_TM), lambda i: (i, 0, 0)),
        out_shape=jax.ShapeDtypeStruct((t // PROJ_TM, DIFF_QK, PROJ_TM), BF16),
        compiler_params=_cparams("parallel"),
        name="kproj",
    )(xb, w, cos_t, sin_t)


def _ret_kernel(dec_ref, qf_ref, kf_ref, vf_ref, qb_ref, kb_ref, vb_ref, dmat_ref, xif_ref, ztf_ref,
                xib_ref, ztb_ref, oa_ref, ob_ref, sf_ref, sb_ref, *, is_first):
    c = pl.program_id(0)

    @pl.when(is_first(c))
    def _():
        sf_ref[...] = jnp.zeros_like(sf_ref)
        sb_ref[...] = jnp.zeros_like(sb_ref)

    for hd in range(R_HEADS):
        qs = slice(hd * R_DK, (hd + 1) * R_DK)
        vs = slice(hd * R_DV, (hd + 1) * R_DV)
        q = qf_ref[:, qs]
        k = kf_ref[:, qs]
        v = vf_ref[:, vs]
        s = lax.dot_general(q, k, NT_DIMS, preferred_element_type=F32)
        p = (s * dmat_ref[hd]).astype(BF16)
        o = jnp.dot(p, v, preferred_element_type=F32)
        st = sf_ref[hd]
        o = o + xif_ref[hd] * jnp.dot(q, st.astype(BF16), preferred_element_type=F32)
        kz = (k.astype(F32) * ztf_ref[hd]).astype(BF16)
        sf_ref[hd] = st * dec_ref[0, hd] + lax.dot_general(kz, v, TN_DIMS, preferred_element_type=F32)
        oa_ref[:, vs] = o
        q = qb_ref[:, qs]
        k = kb_ref[:, qs]
        v = vb_ref[:, vs]
        st = sb_ref[hd]
        ob_ref[:, vs] = xib_ref[hd] * jnp.dot(q, st.astype(BF16), preferred_element_type=F32)
        kz = (k.astype(F32) * ztb_ref[hd]).astype(BF16)
        sb_ref[hd] = st * dec_ref[1, hd] + lax.dot_general(kz, v, TN_DIMS, preferred_element_type=F32)


def _retention(h, log_decay, seqs):
    t = h.shape[0]
    c_len = RET_CHUNK
    n_chunks = t // c_len
    bounds = []
    start = 0
    for nseq, slen in seqs:
        per = slen // c_len
        bounds.append((start, start + nseq * per, per))
        start += nseq * per

    def is_first(c):
        out = False
        for lo, hi, per in bounds:
            out = jnp.logical_or(out, jnp.logical_and(jnp.logical_and(c >= lo, c < hi), (c - lo) % per == 0))
        return out

    def rev(c):
        out = c
        for lo, hi, per in bounds:
            r = lo + ((c - lo) // per) * per + (per - 1 - (c - lo) % per)
            out = jnp.where(jnp.logical_and(c >= lo, c < hi), r, out)
        return out

    lf = -jnp.abs(log_decay[0].astype(F32))
    lb = -jnp.abs(log_decay[1].astype(F32))
    idx = jnp.arange(c_len, dtype=F32)
    diff = idx[:, None] - idx[None, :]
    dmat = jnp.where(diff[None] >= 0,
                     jnp.exp(lf[:, None, None] * jnp.maximum(diff, 0.0)[None]),
                     jnp.exp(lb[:, None, None] * jnp.maximum(-diff, 0.0)[None]))
    ones_v = jnp.ones((1, 1, R_DV), F32)
    ones_k = jnp.ones((1, 1, R_DK), F32)
    xif = jnp.exp(lf[:, None, None] * (idx[None, :, None] + 1.0)) * ones_v
    ztf = jnp.exp(lf[:, None, None] * (c_len - 1.0 - idx[None, :, None])) * ones_k
    xib = jnp.exp(lb[:, None, None] * (c_len - idx[None, :, None])) * ones_v
    ztb = jnp.exp(lb[:, None, None] * idx[None, :, None]) * ones_k
    dec = jnp.stack([jnp.exp(lf * c_len), jnp.exp(lb * c_len)])

    qk_w = RET_QK
    full = lambda shape: pl.BlockSpec(shape, lambda c: (0,) * len(shape))
    kern = functools.partial(_ret_kernel, is_first=is_first)
    return pl.pallas_call(
        kern,
        grid=(n_chunks,),
        in_specs=[
            pl.BlockSpec(memory_space=pltpu.SMEM),
            pl.BlockSpec((c_len, qk_w), lambda c: (c, OFF_RQ // qk_w)),
            pl.BlockSpec((c_len, qk_w), lambda c: (c, OFF_RK // qk_w)),
            pl.BlockSpec((c_len, RET_V), lambda c: (c, OFF_RV // RET_V)),
            pl.BlockSpec((c_len, qk_w), lambda c: (rev(c), OFF_RQ // qk_w)),
            pl.BlockSpec((c_len, qk_w), lambda c: (rev(c), OFF_RK // qk_w)),
            pl.BlockSpec((c_len, RET_V), lambda c: (rev(c), OFF_RV // RET_V)),
            full((R_HEADS, c_len, c_len)),
            full((R_HEADS, c_len, R_DV)),
            full((R_HEADS, c_len, R_DK)),
            full((R_HEADS, c_len, R_DV)),
            full((R_HEADS, c_len, R_DK)),
        ],
        out_specs=[
            pl.BlockSpec((c_len, RET_V), lambda c: (c, 0)),
            pl.BlockSpec((c_len, RET_V), lambda c: (rev(c), 0)),
        ],
        out_shape=[jax.ShapeDtypeStruct((t, RET_V), F32), jax.ShapeDtypeStruct((t, RET_V), F32)],
        scratch_shapes=[pltpu.VMEM((R_HEADS, R_DK, R_DV), F32), pltpu.VMEM((R_HEADS, R_DK, R_DV), F32)],
        compiler_params=_cparams("arbitrary"),
        name="retention",
    )(dec, h, h, h, h, h, h, dmat, xif, ztf, xib, ztb)


ATT_TQ = 512
ATT_TK = 1024
ATT_STRIP = 32
NEG_BIG = -1e30


def _attn_kernel(lam_ref, q_ref, kt_ref, v_ref, gain_ref, o_ref, s_ref, p_ref, m_ref, l_ref, al_ref, acc_ref,
                 *, tk, n_kv, post_scale):
    tq = q_ref.shape[0]
    n_col = tk // LANES
    m_ref[...] = jnp.full(m_ref.shape, NEG_BIG, F32)
    l_ref[...] = jnp.zeros(l_ref.shape, F32)
    acc_ref[...] = jnp.zeros(acc_ref.shape, F32)

    def body(j, carry):
        off = pl.multiple_of(j * tk, tk)
        vv = v_ref[pl.ds(off, tk), :]
        for mp in range(2):
            q = q_ref[:, mp * D_DH:(mp + 1) * D_DH]
            s_ref[mp] = jnp.dot(q, kt_ref[j, mp * D_DH:(mp + 1) * D_DH, :], preferred_element_type=F32)
        for mp in range(2):
            for r in range(tq // ATT_STRIP):
                rows = slice(r * ATT_STRIP, (r + 1) * ATT_STRIP)
                cols = [s_ref[mp, rows, c * LANES:(c + 1) * LANES] for c in range(n_col)]
                smax = functools.reduce(jnp.maximum, cols)
                m_old = m_ref[mp, rows, :]
                m_new = jnp.maximum(m_old, jnp.max(smax, axis=1, keepdims=True))
                alpha = jnp.exp2(m_old - m_new)
                lsum = None
                for c in range(n_col):
                    p = jnp.exp2(cols[c] - m_new)
                    p_ref[mp, rows, c * LANES:(c + 1) * LANES] = p.astype(BF16)
                    lsum = p if lsum is None else lsum + p
                l_ref[mp, rows, :] = alpha * l_ref[mp, rows, :] + lsum
                m_ref[mp, rows, :] = m_new
                al_ref[mp, rows, :] = alpha
        for mp in range(2):
            pv = jnp.dot(p_ref[mp], vv, preferred_element_type=F32)
            al = al_ref[mp]
            acc_ref[mp] = jnp.concatenate([al] * (D_DV // LANES), axis=1) * acc_ref[mp] + pv
        return carry

    lax.fori_loop(0, n_kv, body, 0)
    l1 = jnp.sum(l_ref[0], axis=1, keepdims=True)
    l2 = jnp.sum(l_ref[1], axis=1, keepdims=True)
    out = acc_ref[0] / l1 - lam_ref[0] * (acc_ref[1] / l2)
    ms = jnp.mean(out * out, axis=1, keepdims=True)
    out = out * lax.rsqrt(ms + EPS) * gain_ref[...] * post_scale
    o_ref[...] = out.astype(o_ref.dtype)


def _diff_attention(h, kt, lam, gain, nseq, slen, row0, lam_init):
    tq, tk = ATT_TQ, ATT_TK
    assert row0 % slen == 0 and slen % tk == 0 and kt.shape[2] == tk
    nq = slen // tq
    n_kv = slen // tk
    qb0 = row0 // tq
    sb0 = row0 // slen
    w = 2 * D_DH
    kern = functools.partial(_attn_kernel, tk=tk, n_kv=n_kv, post_scale=1.0 - lam_init)
    return pl.pallas_call(
        kern,
        grid=(nseq, D_HEADS, nq),
        in_specs=[
            pl.BlockSpec(memory_space=pltpu.SMEM),
            pl.BlockSpec((tq, w), lambda b, hd, i: (qb0 + b * nq + i, OFF_DQ // w + hd)),
            pl.BlockSpec((n_kv, w, tk), lambda b, hd, i: (sb0 + b, hd, 0)),
            pl.BlockSpec((slen, D_DV), lambda b, hd, i: (sb0 + b, _h_off(OFF_DV) // D_DV + hd)),
            pl.BlockSpec((1, D_DV), lambda b, hd, i: (0, 0)),
        ],
        out_specs=pl.BlockSpec((tq, D_DV), lambda b, hd, i: (b * nq + i, hd)),
        out_shape=jax.ShapeDtypeStruct((nseq * slen, DIFF_V), BF16),
        scratch_shapes=[
            pltpu.VMEM((2, tq, tk), F32), pltpu.VMEM((2, tq, tk), BF16),
            pltpu.VMEM((2, tq, LANES), F32), pltpu.VMEM((2, tq, LANES), F32), pltpu.VMEM((2, tq, LANES), F32),
            pltpu.VMEM((2, tq, D_DV), F32),
        ],
        compiler_params=_cparams("parallel", "parallel", "parallel"),
        name="diff_attn_s%d" % slen,
    )(lam, h, kt, h, gain)


MERGE_TM = 512


def _layer_norm(y, g, b):
    mu = jnp.mean(y, axis=1, keepdims=True)
    yc = y - mu
    var = jnp.mean(yc * yc, axis=1, keepdims=True)
    return yc * lax.rsqrt(var + EPS) * g + b


def _merge_kernel(x_ref, oa_ref, ob_ref, rg_ref, do_ref, ga_ref, gb_ref, gn_ref, wr_ref, wd_ref, wo_ref,
                  lng_ref, lnb_ref, rw_hi_ref, rw_lo_ref, x1_ref, route_ref):
    ro = oa_ref[...] + ob_ref[...]
    parts = []
    for hd in range(R_HEADS):
        seg = ro[:, hd * R_DV:(hd + 1) * R_DV]
        mu = jnp.mean(seg, axis=1, keepdims=True)
        sc = seg - mu
        var = jnp.mean(sc * sc, axis=1, keepdims=True)
        parts.append(sc * lax.rsqrt(var + EPS))
    ro = jnp.concatenate(parts, axis=1) * gn_ref[...]
    rg = rg_ref[...].astype(F32)
    ret_out = (rg * jax.nn.sigmoid(rg) * ro).astype(BF16)
    a = jnp.dot(ret_out, wr_ref[...], preferred_element_type=F32)
    b = jnp.dot(do_ref[...], wd_ref[...], preferred_element_type=F32)
    merged = jax.nn.sigmoid(ga_ref[...].astype(F32)) * a + jax.nn.sigmoid(gb_ref[...].astype(F32)) * b
    mix = jnp.dot(merged.astype(BF16), wo_ref[...], preferred_element_type=F32)
    x1 = _layer_norm(ALPHA * x_ref[...] + mix, lng_ref[...], lnb_ref[...])
    x1_ref[...] = x1
    x1_hi = x1.astype(BF16)
    x1_lo = (x1 - x1_hi.astype(F32)).astype(BF16)
    logits = (jnp.dot(x1_hi, rw_hi_ref[...], preferred_element_type=F32)
              + jnp.dot(x1_hi, rw_lo_ref[...], preferred_element_type=F32)
              + jnp.dot(x1_lo, rw_hi_ref[...], preferred_element_type=F32))
    lane = lax.broadcasted_iota(jnp.int32, logits.shape, 1)
    is_g = lane < N_GROUPS
    lg = jnp.where(is_g, logits, NEG_BIG)
    gmax = jnp.max(lg, axis=1, keepdims=True)
    gsel = jnp.min(jnp.where(jnp.logical_and(is_g, lg == gmax), lane, LANES), axis=1, keepdims=True)
    gp = 1.0 / jnp.sum(jnp.where(is_g, jnp.exp(lg - gmax), 0.0), axis=1, keepdims=True)
    e_lane = lane - N_GROUPS
    in_grp = jnp.logical_and(jnp.logical_and(e_lane >= 0, e_lane < N_EXPERTS),
                             (e_lane >> 3) == gsel)
    le = jnp.where(in_grp, logits, NEG_BIG)
    m1 = jnp.max(le, axis=1, keepdims=True)
    i1 = jnp.min(jnp.where(jnp.logical_and(in_grp, le == m1), lane, LANES), axis=1, keepdims=True)
    le2 = jnp.where(lane == i1, NEG_BIG, le)
    m2 = jnp.max(le2, axis=1, keepdims=True)
    i2 = jnp.min(jnp.where(jnp.logical_and(in_grp, le2 == m2), lane, LANES), axis=1, keepdims=True)
    e2 = jnp.exp(m2 - m1)
    w1 = gp / (1.0 + e2)
    w2 = gp * e2 / (1.0 + e2)
    e1f = (i1 - N_GROUPS).astype(F32)
    e2f = (i2 - N_GROUPS).astype(F32)
    route_ref[...] = jnp.where(lane == 0, w1, jnp.where(lane == 1, w2, jnp.where(lane == 2, e1f,
                                                                                 jnp.where(lane == 3, e2f, 0.0))))


def _merge(x, oa, ob, h, dout, gn_gain, w_ret, w_diff, w_out, ln_g, ln_b, rw_hi, rw_lo):
    t = x.shape[0]
    tm = MERGE_TM
    row = lambda cb: pl.BlockSpec((tm, D_MODEL), lambda i: (i, cb))
    const = lambda shape: pl.BlockSpec(shape, lambda i: (0,) * len(shape))
    return pl.pallas_call(
        _merge_kernel,
        grid=(t // tm,),
        in_specs=[
            row(0), row(0), row(0),
            row(OFF_RG // D_MODEL),
            row(0),
            row(_h_off(OFF_GA) // D_MODEL),
            row(_h_off(OFF_GB) // D_MODEL),
            const((1, RET_V)),
            const((RET_V, D_MODEL)), const((DIFF_V, D_MODEL)), const((D_MODEL, D_MODEL)),
            const((1, D_MODEL)), const((1, D_MODEL)),
            const((D_MODEL, LANES)), const((D_MODEL, LANES)),
        ],
        out_specs=[row(0), pl.BlockSpec((tm, LANES), lambda i: (i, 0))],
        out_shape=[jax.ShapeDtypeStruct((t, D_MODEL), F32), jax.ShapeDtypeStruct((t, LANES), F32)],
        compiler_params=_cparams("parallel"),
        name="merge",
    )(x, oa, ob, h, dout, h, h, gn_gain, w_ret, w_diff, w_out, ln_g, ln_b, rw_hi, rw_lo)


RANK_TM = 512
EXPERT_TM = 512
SC_ROWS = 32


def _rank_kernel(route_ref, tri_ref, rank_ref, cnt_ref, base_ref):
    @pl.when(pl.program_id(0) == 0)
    def _():
        base_ref[...] = jnp.zeros_like(base_ref)

    r = route_ref[...]
    lane_i = lax.broadcasted_iota(jnp.int32, r.shape, 1)
    lane = lane_i.astype(F32)
    is1 = lane == r[:, 2:3]
    is2 = lane == r[:, 3:4]
    onehot = jnp.where(is1, 1.0, jnp.where(is2, 1.0, 0.0))
    before = jnp.dot(tri_ref[...], onehot.astype(BF16), preferred_element_type=F32) + base_ref[...]
    r1 = jnp.sum(jnp.where(is1, before, 0.0), axis=1, keepdims=True)
    r2 = jnp.sum(jnp.where(is2, before, 0.0), axis=1, keepdims=True)
    rank_ref[...] = jnp.where(lane_i == 0, r1, jnp.where(lane_i == 1, r2, 0.0))
    base_ref[...] += jnp.sum(onehot, axis=0, keepdims=True)
    cnt_ref[...] = jnp.broadcast_to(base_ref[...], cnt_ref.shape)


def _rank(route):
    t = route.shape[0]
    tm = RANK_TM
    idx = jnp.arange(tm)
    tri = (idx[:, None] > idx[None, :]).astype(BF16)
    return pl.pallas_call(
        _rank_kernel,
        grid=(t // tm,),
        in_specs=[pl.BlockSpec((tm, LANES), lambda i: (i, 0)), pl.BlockSpec((tm, tm), lambda i: (0, 0))],
        out_specs=[pl.BlockSpec((tm, LANES), lambda i: (i, 0)), pl.BlockSpec((8, LANES), lambda i: (0, 0))],
        out_shape=[jax.ShapeDtypeStruct((t, LANES), F32), jax.ShapeDtypeStruct((8, LANES), F32)],
        scratch_shapes=[pltpu.VMEM((1, LANES), F32)],
        compiler_params=_cparams("arbitrary"),
        name="rank",
    )(route, tri)


def _sc_mesh():
    return plsc.VectorSubcoreMesh(core_axis_name="c", subcore_axis_name="s")


SC_CORES = 2
SC_SUBCORES = 16


def _sc_worker_rows(n):
    n_workers = SC_CORES * SC_SUBCORES
    assert n % (n_workers * SC_ROWS) == 0
    per_w = n // n_workers
    wid = lambda: lax.axis_index("s") * SC_CORES + lax.axis_index("c")
    return per_w, wid


def _sc_scatter2(x, idx0, idx1, n_out):
    t, d = x.shape
    per_w, wid = _sc_worker_rows(t)

    @functools.partial(
        pl.kernel, out_type=jax.ShapeDtypeStruct((n_out, d), x.dtype), mesh=_sc_mesh(),
        scratch_types=[pltpu.VMEM((SC_ROWS,), jnp.int32), pltpu.VMEM((SC_ROWS,), jnp.int32),
                       pltpu.VMEM((SC_ROWS, d), x.dtype), pltpu.SemaphoreType.DMA])
    def scatter_rows(x_hbm, i0_hbm, i1_hbm, o_hbm, i0_v, i1_v, rows_v, sem):
        base = wid() * per_w

        @pl.loop(0, per_w // SC_ROWS)
        def _(ci):
            off = pl.multiple_of(base + ci * SC_ROWS, SC_ROWS)
            pltpu.sync_copy(i0_hbm.at[pl.ds(off, SC_ROWS)], i0_v)
            pltpu.sync_copy(i1_hbm.at[pl.ds(off, SC_ROWS)], i1_v)
            pltpu.sync_copy(x_hbm.at[pl.ds(off, SC_ROWS)], rows_v)
            pltpu.async_copy(rows_v, o_hbm.at[i0_v], sem).wait()
            pltpu.async_copy(rows_v, o_hbm.at[i1_v], sem).wait()

    return scatter_rows(x, idx0, idx1)


def _sc_gather(table, idx):
    n = idx.shape[0]
    d = table.shape[1]
    per_w, wid = _sc_worker_rows(n)

    @functools.partial(
        pl.kernel, out_type=jax.ShapeDtypeStruct((n, d), table.dtype), mesh=_sc_mesh(),
        scratch_types=[pltpu.VMEM((SC_ROWS,), jnp.int32), pltpu.VMEM((SC_ROWS, d), table.dtype),
                       pltpu.SemaphoreType.DMA])
    def gather_rows(x_hbm, i_hbm, o_hbm, idx_v, rows_v, sem):
        base = wid() * per_w

        @pl.loop(0, per_w // SC_ROWS)
        def _(ci):
            off = pl.multiple_of(base + ci * SC_ROWS, SC_ROWS)
            pltpu.sync_copy(i_hbm.at[pl.ds(off, SC_ROWS)], idx_v)
            pltpu.async_copy(x_hbm.at[idx_v], rows_v, sem).wait()
            pltpu.sync_copy(rows_v, o_hbm.at[pl.ds(off, SC_ROWS)])

    return gather_rows(table, idx)


def _expert_kernel(te_ref, nu_ref, x_ref, wgu_ref, wdn_ref, y_ref):
    @pl.when(pl.program_id(0) < nu_ref[0])
    def _():
        hgu = jnp.dot(x_ref[...].astype(BF16), wgu_ref[0], preferred_element_type=F32)
        a = hgu[:, :D_FF_EXPERT]
        gate = hgu[:, D_FF_EXPERT:]
        act = (a * jax.nn.sigmoid(a) * gate).astype(BF16)
        y_ref[...] = jnp.dot(act, wdn_ref[0], preferred_element_type=F32)


def _experts(xs, tile_expert, n_used, w_gu, w_dn):
    p = xs.shape[0]
    tm = EXPERT_TM
    grid_spec = pltpu.PrefetchScalarGridSpec(
        num_scalar_prefetch=2,
        grid=(p // tm,),
        in_specs=[
            pl.BlockSpec((tm, D_MODEL), lambda i, te, nu: (i, 0)),
            pl.BlockSpec((1, D_MODEL, 2 * D_FF_EXPERT), lambda i, te, nu: (te[i], 0, 0)),
            pl.BlockSpec((1, D_FF_EXPERT, D_MODEL), lambda i, te, nu: (te[i], 0, 0)),
        ],
        out_specs=pl.BlockSpec((tm, D_MODEL), lambda i, te, nu: (i, 0)),
    )
    return pl.pallas_call(
        _expert_kernel,
        grid_spec=grid_spec,
        out_shape=jax.ShapeDtypeStruct((p, D_MODEL), F32),
        compiler_params=_cparams("arbitrary"),
        name="experts",
    )(tile_expert, n_used, xs, w_gu, w_dn)


COMBINE_TM = 512


def _combine_kernel(x_ref, g_ref, route_ref, lng_ref, lnb_ref, o_ref, ob_ref):
    r = route_ref[...]
    y = r[:, 0:1] * g_ref[:, :D_MODEL] + r[:, 1:2] * g_ref[:, D_MODEL:]
    out = _layer_norm(ALPHA * x_ref[...] + y, lng_ref[...], lnb_ref[...])
    o_ref[...] = out
    ob_ref[...] = out.astype(BF16)


def _combine(x1, g, route, ln_g, ln_b):
    t = x1.shape[0]
    tm = COMBINE_TM
    return pl.pallas_call(
        _combine_kernel,
        grid=(t // tm,),
        in_specs=[
            pl.BlockSpec((tm, D_MODEL), lambda i: (i, 0)),
            pl.BlockSpec((tm, 2 * D_MODEL), lambda i: (i, 0)),
            pl.BlockSpec((tm, LANES), lambda i: (i, 0)),
            pl.BlockSpec((1, D_MODEL), lambda i: (0, 0)),
            pl.BlockSpec((1, D_MODEL), lambda i: (0, 0)),
        ],
        out_specs=[pl.BlockSpec((tm, D_MODEL), lambda i: (i, 0)), pl.BlockSpec((tm, D_MODEL), lambda i: (i, 0))],
        out_shape=[jax.ShapeDtypeStruct((t, D_MODEL), F32), jax.ShapeDtypeStruct((t, D_MODEL), BF16)],
        compiler_params=_cparams("parallel"),
        name="combine",
    )(x1, g, route, ln_g, ln_b)


def _moe(x1, route, w_gu, w_dn, ln_g, ln_b):
    t = x1.shape[0]
    te = EXPERT_TM
    n_rows = 2 * t + N_EXPERTS * te
    rank, cnt = _rank(route)
    counts = cnt[0, :N_EXPERTS].astype(jnp.int32)
    padded = ((counts + te - 1) // te) * te
    ends = jnp.cumsum(padded)
    starts = ends - padded
    e1 = route[:, 2].astype(jnp.int32)
    e2 = route[:, 3].astype(jnp.int32)
    pos1 = starts[e1] + rank[:, 0].astype(jnp.int32)
    pos2 = starts[e2] + rank[:, 1].astype(jnp.int32)
    tile_row0 = jnp.arange(n_rows // te, dtype=jnp.int32) * te
    tile_expert = jnp.minimum(jnp.sum(tile_row0[:, None] >= ends[None, :], axis=1), N_EXPERTS - 1).astype(jnp.int32)
    n_used = (ends[-1:] // te).astype(jnp.int32)

    xs = _sc_scatter2(x1, pos1, pos2, n_rows)
    ys = _experts(xs, tile_expert, n_used, w_gu, w_dn)
    g = _sc_gather(ys, jnp.stack([pos1, pos2], axis=1).reshape(2 * t))
    return _combine(x1, g.reshape(t, 2 * D_MODEL), route, ln_g, ln_b)


def _rope_tables(s):
    pos = jnp.arange(s, dtype=F32)
    inv = ROPE_THETA ** (-jnp.arange(0, R_DK, 2, dtype=F32) / R_DK)
    ang = pos[:, None] * inv[None, :]
    cos, sin = jnp.cos(ang), jnp.sin(ang)
    return jnp.concatenate([cos, cos], axis=1), jnp.concatenate([-sin, sin], axis=1)


def kernel(x_prompt, x_sample, w_in, ret_log_decay, ret_gn_gain, diff_lambda, diff_subln_gain, w_ret_branch,
           w_diff_branch, w_out, ln1_g, ln1_b, router_group, router_expert, w_gate_up, w_down, ln2_g, ln2_b):
    bp, sp, _ = x_prompt.shape
    bs, ss, _ = x_sample.shape
    tp, ts = bp * sp, bs * ss
    assert sp % PROJ_TM == 0 and ss % PROJ_TM == 0 and sp % ATT_TQ == 0 and ss % ATT_TQ == 0
    x = jnp.concatenate([x_prompt.reshape(tp, D_MODEL), x_sample.reshape(ts, D_MODEL)], axis=0)

    cos_t, sin_t = _rope_tables(max(sp, ss))
    n_p, per_p, per_s = tp // PROJ_TM, sp // PROJ_TM, ss // PROJ_TM

    def pos_block(i):
        return jnp.where(i < n_p, i % per_p, (i - n_p) % per_s)

    colscale = jnp.ones((IN_WIDTH,), F32)
    colscale = colscale.at[OFF_RK:OFF_RK + RET_QK].set(R_DK ** -0.5)
    colscale = colscale.at[OFF_DQ:OFF_DQ + DIFF_QK].set(math.log2(math.e) / math.sqrt(D_DH))
    colscale = colscale.reshape(1, IN_WIDTH)

    xb = x.astype(BF16)
    for l in range(DEPTH):
        w_in_b = w_in[l].astype(BF16)
        h = _inproj(xb, w_in_b, cos_t, sin_t, colscale, pos_block)
        kt = _kproj(xb, w_in_b, cos_t, sin_t, pos_block)
        oa, ob = _retention(h, ret_log_decay[l], ((bp, sp), (bs, ss)))
        lam_init = 0.8 - 0.6 * math.exp(-0.3 * l)
        lf = diff_lambda[l].astype(F32)
        lam = (jnp.exp(jnp.sum(lf[0] * lf[1])) - jnp.exp(jnp.sum(lf[2] * lf[3])) + lam_init).reshape(1)
        gain = diff_subln_gain[l].astype(F32).reshape(1, D_DV)
        dout = jnp.concatenate([
            _diff_attention(h, kt, lam, gain, bp, sp, 0, lam_init),
            _diff_attention(h, kt, lam, gain, bs, ss, tp, lam_init),
        ], axis=0)
        rw = jnp.concatenate([router_group[l], router_expert[l],
                              jnp.zeros((D_MODEL, LANES - N_GROUPS - N_EXPERTS), F32)], axis=1).astype(F32)
        rw_hi = rw.astype(BF16)
        rw_lo = (rw - rw_hi.astype(F32)).astype(BF16)
        x1, route = _merge(x, oa, ob, h, dout, ret_gn_gain[l].astype(F32).reshape(1, RET_V),
                           w_ret_branch[l].astype(BF16), w_diff_branch[l].astype(BF16),
                           w_out[l].astype(BF16), ln1_g[l].reshape(1, D_MODEL), ln1_b[l].reshape(1, D_MODEL),
                           rw_hi, rw_lo)
        x, xb = _moe(x1, route, w_gate_up[l].astype(BF16), w_down[l].astype(BF16),
                     ln2_g[l].reshape(1, D_MODEL), ln2_b[l].reshape(1, D_MODEL))
    return x[:tp].reshape(bp, sp, D_MODEL), x[tp:].reshape(bs, ss, D_MODEL)
```
